```python
import math
import jax, jax.numpy as jnp
from jax import lax
import numpy as np

D_MODEL = 1024
BATCH = 8
SEQ = 2048
DEPTH = 2
DEC_BATCH = 32
DEC_SEQ = 1
PAST_LEN = 16384
PAGE_SIZE = 128

N_ATT_LAYERS = (DEPTH + 1) // 2
N_CONV_LAYERS = DEPTH // 2
ATT_WIDTH = D_MODEL // 2
N_ATT_HEADS = 4
ATT_HD = ATT_WIDTH // (2 * N_ATT_HEADS)
Q_BLOCK = 128
N_BUCKETS = 32
MAX_DISTANCE = 128
RG_WIDTH = D_MODEL // 2
RG_BLOCKS = 8
RG_BD = RG_WIDTH // RG_BLOCKS
RG_CONV = 4
RG_C = 8.0
SC_WIDTH = D_MODEL
SC_CONV = 3
N_KEYS = 128
N_EXPERTS = N_KEYS * N_KEYS
PEER_HEADS = 8
PEER_KDIM = 256
PEER_TOPK = 16
PEER_BLOCK = 256
EPS = 1e-6
SPLIT_A = (ATT_WIDTH, 2 * ATT_WIDTH, 3 * ATT_WIDTH, 3 * ATT_WIDTH + RG_WIDTH)
IN_A_WIDTH = 3 * ATT_WIDTH + 2 * RG_WIDTH

kernel_name = "hybrid_diffattn_rglru_shortconv_peer_step"


def lambda_init(layer):
    return 0.8 - 0.6 * math.exp(-0.3 * layer)


def rmsnorm(x, g):
    xf = x.astype(jnp.float32)
    y = xf * lax.rsqrt(jnp.mean(xf * xf, axis=-1, keepdims=True) + EPS)
    return (y * g.astype(jnp.float32)).astype(x.dtype)


def t5_bucket(dist):
    max_exact = N_BUCKETS // 2
    n = jnp.maximum(dist, 0)
    nf = jnp.maximum(n, 1).astype(jnp.float32)
    far = max_exact + (jnp.log(nf / max_exact) / math.log(MAX_DISTANCE / max_exact)
                       * (N_BUCKETS - max_exact)).astype(jnp.int32)
    return jnp.where(n < max_exact, n, jnp.minimum(far, N_BUCKETS - 1))


def rel_bias_for(rel_bias, dist):
    return jnp.moveaxis(rel_bias[t5_bucket(dist)], -1, 0).astype(jnp.float32)


def diff_lambda(lq1, lk1, lq2, lk2, lam_init):
    f32 = jnp.float32
    return (jnp.exp(jnp.sum(lq1.astype(f32) * lk1.astype(f32)))
            - jnp.exp(jnp.sum(lq2.astype(f32) * lk2.astype(f32))) + lam_init)


def diff_weights(logits, lam):
    p = jax.nn.softmax(logits, axis=-1)
    return p[:, :, 0] - lam * p[:, :, 1]


def diff_attn_prompt(q, k, v, rel_bias, lam):
    b, s = q.shape[0], q.shape[1]
    nblk = s // Q_BLOCK
    scale = ATT_HD ** -0.5
    qb = jnp.moveaxis(q.reshape(b, nblk, Q_BLOCK, N_ATT_HEADS, 2, ATT_HD), 1, 0)
    kpos = jnp.arange(s, dtype=jnp.int32)

    def block(args):
        qi, i = args
        qpos = i * Q_BLOCK + jnp.arange(Q_BLOCK, dtype=jnp.int32)
        dist = qpos[:, None] - kpos[None, :]
        logits = jnp.einsum('bqhcd,bkhcd->bhcqk', qi, k).astype(jnp.float32) * scale
        logits = logits + rel_bias_for(rel_bias, dist)[None, :, None]
        logits = jnp.where(dist >= 0, logits, -jnp.inf)
        w = diff_weights(logits, lam).astype(v.dtype)
        return jnp.einsum('bhqk,bkhe->bqhe', w, v)

    out = lax.map(block, (qb, jnp.arange(nblk, dtype=jnp.int32)))
    return jnp.moveaxis(out, 0, 1).reshape(b, s, N_ATT_HEADS, 2 * ATT_HD)


def diff_attn_sample(q, k_new, v_new, k_past, v_past, rel_bias, lam):
    t, p = q.shape[1], k_past.shape[1]
    scale = ATT_HD ** -0.5
    qpos = p + jnp.arange(t, dtype=jnp.int32)
    dist_past = qpos[:, None] - jnp.arange(p, dtype=jnp.int32)[None, :]
    dist_new = qpos[:, None] - qpos[None, :]
    lp = (jnp.einsum('bqhcd,bkhcd->bhcqk', q, k_past).astype(jnp.float32) * scale
          + rel_bias_for(rel_bias, dist_past)[None, :, None])
    ln = (jnp.einsum('bqhcd,bkhcd->bhcqk', q, k_new).astype(jnp.float32) * scale
          + rel_bias_for(rel_bias, dist_new)[None, :, None])
    ln = jnp.where(dist_new >= 0, ln, -jnp.inf)
    w = diff_weights(jnp.concatenate([lp, ln], axis=-1), lam).astype(v_new.dtype)
    return (jnp.einsum('bhqk,bkhe->bqhe', w[..., :p], v_past)
            + jnp.einsum('bhqk,bkhe->bqhe', w[..., p:], v_new))


def causal_dwconv(x, buf, w, b=None):
    width, t = w.shape[0], x.shape[1]
    xp = jnp.concatenate([buf.astype(x.dtype), x], axis=1)
    y = xp[:, 0:t] * w[0]
    for i in range(1, width):
        y = y + xp[:, i:i + t] * w[i]
    if b is not None:
        y = y + b
    return y, xp[:, t:]


def rglru(xc, h0, wa, ba, wi, bi, lam):
    f32 = jnp.float32
    b, t, _ = xc.shape
    xf = xc.astype(f32)
    xb = xf.reshape(b, t, RG_BLOCKS, RG_BD)
    r = jax.nn.sigmoid(jnp.einsum('btnd,nde->btne', xb, wa.astype(f32)).reshape(b, t, RG_WIDTH) + ba.astype(f32))
    i = jax.nn.sigmoid(jnp.einsum('btnd,nde->btne', xb, wi.astype(f32)).reshape(b, t, RG_WIDTH) + bi.astype(f32))
    log_a = -RG_C * r * jax.nn.softplus(-lam.astype(f32))
    a = jnp.exp(log_a)
    u = jnp.sqrt(-jnp.expm1(2.0 * log_a)) * (i * xf)
    u = u.at[:, 0].add(a[:, 0] * h0.astype(f32))

    def combine(c1, c2):
        return c1[0] * c2[0], c2[0] * c1[1] + c2[1]

    _, h = lax.associative_scan(combine, (a, u), axis=1)
    return h.astype(xc.dtype), h[:, -1].astype(xc.dtype)


def peer(x, w_q, sub_keys, u_tab, v_tab):
    shp = x.shape
    xf = x.reshape(-1, D_MODEL)
    n_tok = xf.shape[0]
    blk = min(PEER_BLOCK, n_tok)
    nblk = -(-n_tok // blk)
    xf = jnp.pad(xf, ((0, nblk * blk - n_tok), (0, 0)))

    def one(xb):
        q = (xb @ w_q).reshape(blk, PEER_HEADS, 2, PEER_KDIM // 2)
        s = jnp.einsum('thcd,hcnd->thcn', q, sub_keys).astype(jnp.float32)
        s1, i1 = lax.top_k(s[:, :, 0], PEER_TOPK)
        s2, i2 = lax.top_k(s[:, :, 1], PEER_TOPK)
        cand = (s1[..., :, None] + s2[..., None, :]).reshape(blk, PEER_HEADS, PEER_TOPK * PEER_TOPK)
        sc, ci = lax.top_k(cand, PEER_TOPK)
        e1 = jnp.take_along_axis(i1, ci // PEER_TOPK, axis=-1)
        e2 = jnp.take_along_axis(i2, ci % PEER_TOPK, axis=-1)
        eidx = e1 * N_KEYS + e2
        g = jax.nn.softmax(sc, axis=-1)
        act = jax.nn.gelu(jnp.einsum('td,thkd->thk', xb, u_tab[eidx]).astype(jnp.float32))
        coef = (g * act).astype(xb.dtype)
        return jnp.einsum('thk,thkd->td', coef, v_tab[eidx])

    y = lax.map(one, xf.reshape(nblk, blk, D_MODEL))
    return y.reshape(-1, D_MODEL)[:n_tok].reshape(shp)


def setup_inputs(seed: int = 0) -> dict:
    key = jax.random.key(seed)
    ks = iter(jax.random.split(key, 48))
    f32 = jnp.float32

    def nrm(shape, scale):
        return jax.random.normal(next(ks), shape, f32) * scale

    def gain(shape):
        return 1.0 + nrm(shape, 0.05)

    n_pages = PAST_LEN // PAGE_SIZE
    n_used = DEC_BATCH * n_pages
    n_phys = n_used + (n_used + 3) // 4
    perm = jax.random.permutation(next(ks), n_phys).astype(jnp.int32)
    page_table = perm[:n_used].reshape(DEC_BATCH, n_pages)
    a_c = jax.random.uniform(next(ks), (N_ATT_LAYERS, RG_WIDTH), f32, 0.9, 0.999)
    s = a_c ** (1.0 / RG_C)
    rg_lambda = jnp.log(s) - jnp.log1p(-s)
    kv_shape = (N_ATT_LAYERS, n_phys, PAGE_SIZE, N_ATT_HEADS, 2 * ATT_HD)
    return {
        "x_prompt": nrm((BATCH, SEQ, D_MODEL), 1.0),
        "x_sample": nrm((DEC_BATCH, DEC_SEQ, D_MODEL), 1.0),
        "cache_k": nrm(kv_shape, 1.0),
        "cache_v": nrm(kv_shape, 1.0),
        "page_table": page_table,
        "state_rglru_h": nrm((N_ATT_LAYERS, DEC_BATCH, RG_WIDTH), 0.5),
        "state_rglru_conv": nrm((N_ATT_LAYERS, DEC_BATCH, RG_CONV - 1, RG_WIDTH), 1.0),
        "state_sconv": nrm((N_CONV_LAYERS, DEC_BATCH, SC_CONV - 1, SC_WIDTH), 1.0),
        "rel_bias": nrm((N_BUCKETS, N_ATT_HEADS), 0.5),
        "norm_mix": gain((DEPTH, D_MODEL)),
        "norm_ffn": gain((DEPTH, D_MODEL)),
        "norm_final": gain((D_MODEL,)),
        "w_in_a": nrm((N_ATT_LAYERS, D_MODEL, IN_A_WIDTH), D_MODEL ** -0.5),
        "w_out_a": nrm((N_ATT_LAYERS, ATT_WIDTH + RG_WIDTH, D_MODEL), (ATT_WIDTH + RG_WIDTH) ** -0.5),
        "lambda_q1": nrm((N_ATT_LAYERS, ATT_HD), 0.1),
        "lambda_k1": nrm((N_ATT_LAYERS, ATT_HD), 0.1),
        "lambda_q2": nrm((N_ATT_LAYERS, ATT_HD), 0.1),
        "lambda_k2": nrm((N_ATT_LAYERS, ATT_HD), 0.1),
        "subln_g": gain((N_ATT_LAYERS, 2 * ATT_HD)),
        "rg_conv_w": nrm((N_ATT_LAYERS, RG_CONV, RG_WIDTH), RG_CONV ** -0.5),
        "rg_conv_b": nrm((N_ATT_LAYERS, RG_WIDTH), 0.02),
        "rg_wa": nrm((N_ATT_LAYERS, RG_BLOCKS, RG_BD, RG_BD), RG_BD ** -0.5),
        "rg_ba": nrm((N_ATT_LAYERS, RG_WIDTH), 0.1),
        "rg_wi": nrm((N_ATT_LAYERS, RG_BLOCKS, RG_BD, RG_BD), RG_BD ** -0.5),
        "rg_bi": nrm((N_ATT_LAYERS, RG_WIDTH), 0.1),
        "rg_lambda": rg_lambda,
        "w_in_c": nrm((N_CONV_LAYERS, D_MODEL, 3 * SC_WIDTH), D_MODEL ** -0.5),
        "sc_conv_w": nrm((N_CONV_LAYERS, SC_CONV, SC_WIDTH), SC_CONV ** -0.5),
        "w_out_c": nrm((N_CONV_LAYERS, SC_WIDTH, D_MODEL), SC_WIDTH ** -0.5),
        "peer_wq": nrm((DEPTH, D_MODEL, PEER_HEADS * PEER_KDIM), D_MODEL ** -0.5),
        "peer_keys": nrm((DEPTH, PEER_HEADS, 2, N_KEYS, PEER_KDIM // 2), (PEER_KDIM // 2) ** -0.5),
        "peer_u": nrm((DEPTH, N_EXPERTS, D_MODEL), D_MODEL ** -0.5),
        "peer_v": nrm((DEPTH, N_EXPERTS, D_MODEL), 0.5 / math.sqrt(PEER_HEADS)),
    }


def reference(x_prompt, x_sample, cache_k, cache_v, page_table, state_rglru_h, state_rglru_conv, state_sconv,
              rel_bias, norm_mix, norm_ffn, norm_final,
              w_in_a, w_out_a, lambda_q1, lambda_k1, lambda_q2, lambda_k2, subln_g,
              rg_conv_w, rg_conv_b, rg_wa, rg_ba, rg_wi, rg_bi, rg_lambda,
              w_in_c, sc_conv_w, w_out_c,
              peer_wq, peer_keys, peer_u, peer_v):

    def run(x, is_sample):
        nb, t = x.shape[0], x.shape[1]
        ks, vs, hs, rbufs, sbufs = [], [], [], [], []
        for l in range(DEPTH):
            j = l // 2
            h = rmsnorm(x, norm_mix[l])
            if l % 2 == 0:
                z = h @ w_in_a[j]
                q, k, v, xr, gr = jnp.split(z, SPLIT_A, axis=-1)
                q = q.reshape(nb, t, N_ATT_HEADS, 2, ATT_HD)
                k = k.reshape(nb, t, N_ATT_HEADS, 2, ATT_HD)
                v = v.reshape(nb, t, N_ATT_HEADS, 2 * ATT_HD)
                lam = diff_lambda(lambda_q1[j], lambda_k1[j], lambda_q2[j], lambda_k2[j], lambda_init(l))
                if is_sample:
                    k_past = cache_k[j, page_table].reshape(nb, -1, N_ATT_HEADS, 2, ATT_HD)
                    v_past = cache_v[j, page_table].reshape(nb, -1, N_ATT_HEADS, 2 * ATT_HD)
                    o = diff_attn_sample(q, k, v, k_past, v_past, rel_bias, lam)
                    buf0 = state_rglru_conv[j]
                    h0 = state_rglru_h[j]
                else:
                    o = diff_attn_prompt(q, k, v, rel_bias, lam)
                    buf0 = jnp.zeros((nb, RG_CONV - 1, RG_WIDTH), x.dtype)
                    h0 = jnp.zeros((nb, RG_WIDTH), x.dtype)
                o = rmsnorm(o, subln_g[j]) * (1.0 - lambda_init(l))
                xc, rbuf = causal_dwconv(xr, buf0, rg_conv_w[j], rg_conv_b[j])
                hr, h_last = rglru(xc, h0, rg_wa[j], rg_ba[j], rg_wi[j], rg_bi[j], rg_lambda[j])
                rg = hr * jax.nn.gelu(gr)
                mix = jnp.concatenate([o.reshape(nb, t, ATT_WIDTH), rg], axis=-1) @ w_out_a[j]
                ks.append(k.reshape(nb, t, N_ATT_HEADS, 2 * ATT_HD))
                vs.append(v)
                hs.append(h_last)
                rbufs.append(rbuf)
            else:
                z = h @ w_in_c[j]
                bg, cg, xs = jnp.split(z, 3, axis=-1)
                if is_sample:
                    buf0 = state_sconv[j]
                else:
                    buf0 = jnp.zeros((nb, SC_CONV - 1, SC_WIDTH), x.dtype)
                cv, sbuf = causal_dwconv(cg * xs, buf0, sc_conv_w[j])
                mix = (bg * cv) @ w_out_c[j]
                sbufs.append(sbuf)
            x = x + mix
            x = x + peer(rmsnorm(x, norm_ffn[l]), peer_wq[l], peer_keys[l], peer_u[l], peer_v[l])
        return (rmsnorm(x, norm_final), jnp.stack(ks), jnp.stack(vs), jnp.stack(hs),
                jnp.stack(rbufs), jnp.stack(sbufs))

    y_prompt, k_prompt, v_prompt, h_prompt, rgconv_prompt, sconv_prompt = run(x_prompt, False)
    y_sample, k_sample, v_sample, h_sample, rgconv_sample, sconv_sample = run(x_sample, True)
    return (y_prompt, y_sample, k_prompt, v_prompt, k_sample, v_sample, h_prompt, h_sample,
            rgconv_prompt, rgconv_sample, sconv_prompt, sconv_sample)
```

```python
import functools
import math

import jax
import jax.numpy as jnp
from jax import lax
from jax.experimental import pallas as pl
from jax.experimental.pallas import tpu as pltpu

F32 = jnp.float32
BF16 = jnp.bfloat16

D_MODEL = 1024
DEPTH = 2
PAGE_SIZE = 128
ATT_WIDTH = D_MODEL // 2
N_ATT_HEADS = 4
ATT_HD = ATT_WIDTH // (2 * N_ATT_HEADS)
Q_BLOCK = 128
N_BUCKETS = 32
MAX_DISTANCE = 128
RG_WIDTH = D_MODEL // 2
RG_BLOCKS = 8
RG_BD = RG_WIDTH // RG_BLOCKS
RG_C = 8.0
SC_WIDTH = D_MODEL
N_KEYS = 128
PEER_HEADS = 8
PEER_KDIM = 256
PEER_TOPK = 16
PEER_BLOCK = 256
EPS = 1e-6
SPLIT_A = (ATT_WIDTH, 2 * ATT_WIDTH, 3 * ATT_WIDTH, 3 * ATT_WIDTH + RG_WIDTH)

VMEM_LIMIT = 56 * 1024 * 1024
ROW_BLOCK = 512
COL_CHUNK = 512


def _lambda_init(layer):
    return 0.8 - 0.6 * math.exp(-0.3 * layer)


def _rmsnorm(x, g):
    xf = x.astype(F32)
    y = xf * lax.rsqrt(jnp.mean(xf * xf, axis=-1, keepdims=True) + EPS)
    return (y * g.astype(F32)).astype(x.dtype)


def _norm_matmul_kernel(x_ref, g_ref, w_ref, o_ref):
    x = x_ref[...]
    xn = (x * lax.rsqrt(jnp.mean(x * x, axis=-1, keepdims=True) + EPS) * g_ref[...]).astype(BF16)
    for j in range(o_ref.shape[1] // COL_CHUNK):
        sl = slice(j * COL_CHUNK, (j + 1) * COL_CHUNK)
        o_ref[:, sl] = jnp.dot(xn, w_ref[:, sl], preferred_element_type=F32)


def norm_matmul(x, g, w):
    n, d = x.shape
    m = w.shape[1]
    rb = min(ROW_BLOCK, n)
    assert n % rb == 0 and m % COL_CHUNK == 0
    return pl.pallas_call(
        _norm_matmul_kernel,
        grid=(n // rb,),
        in_specs=[pl.BlockSpec((rb, d), lambda i: (i, 0)),
                  pl.BlockSpec((1, d), lambda i: (0, 0)),
                  pl.BlockSpec((d, m), lambda i: (0, 0))],
        out_specs=pl.BlockSpec((rb, m), lambda i: (i, 0)),
        out_shape=jax.ShapeDtypeStruct((n, m), F32),
        compiler_params=pltpu.CompilerParams(dimension_semantics=("parallel",),
                                             vmem_limit_bytes=VMEM_LIMIT),
        name="norm_matmul",
    )(x, g.reshape(1, d), w.astype(BF16))


def _matmul_residual_kernel(a_ref, w_ref, x_ref, o_ref):
    o_ref[...] = x_ref[...] + jnp.dot(a_ref[...].astype(BF16), w_ref[...], preferred_element_type=F32)


def matmul_residual(a, w, x):
    n, d = x.shape
    ka = a.shape[1]
    rb = min(ROW_BLOCK, n)
    assert n % rb == 0
    return pl.pallas_call(
        _matmul_residual_kernel,
        grid=(n // rb,),
        in_specs=[pl.BlockSpec((rb, ka), lambda i: (i, 0)),
                  pl.BlockSpec((ka, d), lambda i: (0, 0)),
                  pl.BlockSpec((rb, d), lambda i: (i, 0))],
        out_specs=pl.BlockSpec((rb, d), lambda i: (i, 0)),
        out_shape=jax.ShapeDtypeStruct((n, d), F32),
        compiler_params=pltpu.CompilerParams(dimension_semantics=("parallel",),
                                             vmem_limit_bytes=VMEM_LIMIT),
        name="matmul_residual",
    )(a, w.astype(BF16), x)


def _t5_bucket(dist):
    max_exact = N_BUCKETS // 2
    n = jnp.maximum(dist, 0)
    nf = jnp.maximum(n, 1).astype(F32)
    far = max_exact + (jnp.log(nf / max_exact) / math.log(MAX_DISTANCE / max_exact)
                       * (N_BUCKETS - max_exact)).astype(jnp.int32)
    return jnp.where(n < max_exact, n, jnp.minimum(far, N_BUCKETS - 1))


def _rel_bias_for(rel_bias, dist):
    return jnp.moveaxis(rel_bias[_t5_bucket(dist)], -1, 0).astype(F32)


def _diff_lambda(lq1, lk1, lq2, lk2, lam_init):
    return (jnp.exp(jnp.sum(lq1.astype(F32) * lk1.astype(F32)))
            - jnp.exp(jnp.sum(lq2.astype(F32) * lk2.astype(F32))) + lam_init)


def _diff_weights(logits, lam):
    p = jax.nn.softmax(logits, axis=-1)
    return p[:, :, 0] - lam * p[:, :, 1]


def _diff_attn_prompt(q, k, v, rel_bias, lam):
    b, s = q.shape[0], q.shape[1]
    nblk = s // Q_BLOCK
    scale = ATT_HD ** -0.5
    qb = jnp.moveaxis(q.reshape(b, nblk, Q_BLOCK, N_ATT_HEADS, 2, ATT_HD), 1, 0)
    kpos = jnp.arange(s, dtype=jnp.int32)

    def block(args):
        qi, i = args
        qpos = i * Q_BLOCK + jnp.arange(Q_BLOCK, dtype=jnp.int32)
        dist = qpos[:, None] - kpos[None, :]
        logits = jnp.einsum('bqhcd,bkhcd->bhcqk', qi, k).astype(F32) * scale
        logits = logits + _rel_bias_for(rel_bias, dist)[None, :, None]
        logits = jnp.where(dist >= 0, logits, -jnp.inf)
        w = _diff_weights(logits, lam).astype(v.dtype)
        return jnp.einsum('bhqk,bkhe->bqhe', w, v)

    out = lax.map(block, (qb, jnp.arange(nblk, dtype=jnp.int32)))
    return jnp.moveaxis(out, 0, 1).reshape(b, s, N_ATT_HEADS, 2 * ATT_HD)


def _diff_attn_sample(q, k_new, v_new, k_past, v_past, rel_bias, lam):
    t, p = q.shape[1], k_past.shape[1]
    scale = ATT_HD ** -0.5
    qpos = p + jnp.arange(t, dtype=jnp.int32)
    dist_past = qpos[:, None] - jnp.arange(p, dtype=jnp.int32)[None, :]
    dist_new = qpos[:, None] - qpos[None, :]
    lp = (jnp.einsum('bqhcd,bkhcd->bhcqk', q, k_past).astype(F32) * scale
          + _rel_bias_for(rel_bias, dist_past)[None, :, None])
    ln = (jnp.einsum('bqhcd,bkhcd->bhcqk', q, k_new).astype(F32) * scale
          + _rel_bias_for(rel_bias, dist_new)[None, :, None])
    ln = jnp.where(dist_new >= 0, ln, -jnp.inf)
    w = _diff_weights(jnp.concatenate([lp, ln], axis=-1), lam).astype(v_new.dtype)
    return (jnp.einsum('bhqk,bkhe->bqhe', w[..., :p], v_past)
            + jnp.einsum('bhqk,bkhe->bqhe', w[..., p:], v_new))


def _causal_dwconv(x, buf, w, b=None):
    width, t = w.shape[0], x.shape[1]
    xp = jnp.concatenate([buf.astype(x.dtype), x], axis=1)
    y = xp[:, 0:t] * w[0]
    for i in range(1, width):
        y = y + xp[:, i:i + t] * w[i]
    if b is not None:
        y = y + b
    return y, xp[:, t:]


def _rglru(xc, h0, wa, ba, wi, bi, lam):
    b, t, _ = xc.shape
    xf = xc.astype(F32)
    xb = xf.reshape(b, t, RG_BLOCKS, RG_BD)
    r = jax.nn.sigmoid(jnp.einsum('btnd,nde->btne', xb, wa.astype(F32)).reshape(b, t, RG_WIDTH) + ba.astype(F32))
    i = jax.nn.sigmoid(jnp.einsum('btnd,nde->btne', xb, wi.astype(F32)).reshape(b, t, RG_WIDTH) + bi.astype(F32))
    log_a = -RG_C * r * jax.nn.softplus(-lam.astype(F32))
    a = jnp.exp(log_a)
    u = jnp.sqrt(-jnp.expm1(2.0 * log_a)) * (i * xf)
    u = u.at[:, 0].add(a[:, 0] * h0.astype(F32))

    def combine(c1, c2):
        return c1[0] * c2[0], c2[0] * c1[1] + c2[1]

    _, h = lax.associative_scan(combine, (a, u), axis=1)
    return h.astype(xc.dtype), h[:, -1].astype(xc.dtype)


def _peer_from_queries(xn, q_all, sub_keys, u_tab, v_tab):
    n_tok = xn.shape[0]
    blk = min(PEER_BLOCK, n_tok)
    nblk = -(-n_tok // blk)
    pad = ((0, nblk * blk - n_tok), (0, 0))
    xn = jnp.pad(xn, pad)
    q_all = jnp.pad(q_all, pad)

    def one(args):
        xb, qb = args
        q = qb.reshape(blk, PEER_HEADS, 2, PEER_KDIM // 2)
        s = jnp.einsum('thcd,hcnd->thcn', q, sub_keys).astype(F32)
        s1, i1 = lax.top_k(s[:, :, 0], PEER_TOPK)
        s2, i2 = lax.top_k(s[:, :, 1], PEER_TOPK)
        cand = (s1[..., :, None] + s2[..., None, :]).reshape(blk, PEER_HEADS, PEER_TOPK * PEER_TOPK)
        sc, ci = lax.top_k(cand, PEER_TOPK)
        e1 = jnp.take_along_axis(i1, ci // PEER_TOPK, axis=-1)
        e2 = jnp.take_along_axis(i2, ci % PEER_TOPK, axis=-1)
        eidx = e1 * N_KEYS + e2
        g = jax.nn.softmax(sc, axis=-1)
        act = jax.nn.gelu(jnp.einsum('td,thkd->thk', xb, u_tab[eidx]).astype(F32))
        coef = (g * act).astype(xb.dtype)
        return jnp.einsum('thk,thkd->td', coef, v_tab[eidx])

    y = lax.map(one, (xn.reshape(nblk, blk, D_MODEL), q_all.reshape(nblk, blk, -1)))
    return y.reshape(-1, D_MODEL)[:n_tok]


def kernel(x_prompt, x_sample, cache_k, cache_v, page_table, state_rglru_h, state_rglru_conv, state_sconv, rel_bias, norm_mix, norm_ffn, norm_final, w_in_a, w_out_a, lambda_q1, lambda_k1, lambda_q2, lambda_k2, subln_g, rg_conv_w, rg_conv_b, rg_wa, rg_ba, rg_wi, rg_bi, rg_lambda, w_in_c, sc_conv_w, w_out_c, peer_wq, peer_keys, peer_u, peer_v):

    def run(x, is_sample):
        nb, t = x.shape[0], x.shape[1]
        n = nb * t
        xf = x.reshape(n, D_MODEL)
        ks, vs, hs, rbufs, sbufs = [], [], [], [], []
        for l in range(DEPTH):
            j = l // 2
            if l % 2 == 0:
                z = norm_matmul(xf, norm_mix[l], w_in_a[j]).reshape(nb, t, -1)
                q, k, v, xr, gr = jnp.split(z, SPLIT_A, axis=-1)
                q = q.reshape(nb, t, N_ATT_HEADS, 2, ATT_HD)
                k = k.reshape(nb, t, N_ATT_HEADS, 2, ATT_HD)
                v = v.reshape(nb, t, N_ATT_HEADS, 2 * ATT_HD)
                lam = _diff_lambda(lambda_q1[j], lambda_k1[j], lambda_q2[j], lambda_k2[j], _lambda_init(l))
                if is_sample:
                    k_past = cache_k[j, page_table].reshape(nb, -1, N_ATT_HEADS, 2, ATT_HD)
                    v_past = cache_v[j, page_table].reshape(nb, -1, N_ATT_HEADS, 2 * ATT_HD)
                    o = _diff_attn_sample(q, k, v, k_past, v_past, rel_bias, lam)
                    buf0 = state_rglru_conv[j]
                    h0 = state_rglru_h[j]
                else:
                    o = _diff_attn_prompt(q, k, v, rel_bias, lam)
                    buf0 = jnp.zeros((nb, rg_conv_w.shape[1] - 1, RG_WIDTH), x.dtype)
                    h0 = jnp.zeros((nb, RG_WIDTH), x.dtype)
                o = _rmsnorm(o, subln_g[j]) * (1.0 - _lambda_init(l))
                xc, rbuf = _causal_dwconv(xr, buf0, rg_conv_w[j], rg_conv_b[j])
                hr, h_last = _rglru(xc, h0, rg_wa[j], rg_ba[j], rg_wi[j], rg_bi[j], rg_lambda[j])
                rg = hr * jax.nn.gelu(gr)
                mix_in = jnp.concatenate([o.reshape(nb, t, ATT_WIDTH), rg], axis=-1).reshape(n, -1)
                xf = matmul_residual(mix_in, w_out_a[j], xf)
                ks.append(k.reshape(nb, t, N_ATT_HEADS, 2 * ATT_HD))
                vs.append(v)
                hs.append(h_last)
                rbufs.append(rbuf)
            else:
                z = norm_matmul(xf, norm_mix[l], w_in_c[j]).reshape(nb, t, -1)
                bg, cg, xs = jnp.split(z, 3, axis=-1)
                if is_sample:
                    buf0 = state_sconv[j]
                else:
                    buf0 = jnp.zeros((nb, sc_conv_w.shape[1] - 1, SC_WIDTH), x.dtype)
                cv, sbuf = _causal_dwconv(cg * xs, buf0, sc_conv_w[j])
                xf = matmul_residual((bg * cv).reshape(n, -1), w_out_c[j], xf)
                sbufs.append(sbuf)
            xn = _rmsnorm(xf, norm_ffn[l])
            q_all = norm_matmul(xf, norm_ffn[l], peer_wq[l])
            xf = xf + _peer_from_queries(xn, q_all, peer_keys[l], peer_u[l], peer_v[l])
        y = _rmsnorm(xf, norm_final).reshape(nb, t, D_MODEL)
        return (y, jnp.stack(ks), jnp.stack(vs), jnp.stack(hs), jnp.stack(rbufs), jnp.stack(sbufs))

    y_prompt, k_prompt, v_prompt, h_prompt, rgconv_prompt, sconv_prompt = run(x_prompt, False)
    y_sample, k_sample, v_sample, h_sample, rgconv_sample, sconv_sample = run(x_sample, True)
    return (y_prompt, y_sample, k_prompt, v_prompt, k_sample, v_sample, h_prompt, h_sample,
            rgconv_prompt, rgconv_sample, sconv_prompt, sconv_sample)
```
